```python
import math
import jax, jax.numpy as jnp
from jax import lax
import numpy as np

D_MODEL = 2048
BATCH = 4
SEQ = 8192
DEPTH = 4

S5_WIDTH = 768
S5_GROUP = 16
S5_GROUPS = S5_WIDTH // S5_GROUP
S5_STATE = 64
SSD_HEADS = 16
SSD_HEAD_DIM = 64
SSD_WIDTH = SSD_HEADS * SSD_HEAD_DIM
SSD_GROUPS = 4
SSD_STATE = 128
SSD_CONV = 4
SSD_CHUNK = 128
SSD_CONV_CH = SSD_WIDTH + 2 * SSD_GROUPS * SSD_STATE
LRU_WIDTH = 1024
LRU_BLOCKS = 8
LRU_BLOCK = LRU_WIDTH // LRU_BLOCKS
LRU_CONV = 4
LRU_C = 8.0
RET_HEADS = 8
RET_KEY_DIM = 64
RET_VAL_DIM = 128
RET_QK_WIDTH = RET_HEADS * RET_KEY_DIM
RET_WIDTH = RET_HEADS * RET_VAL_DIM
RET_CHUNK = 128
ROPE_BASE = 10000.0
N_BRANCH = 4
D_FF = 5632
FFN_CONV = 3
N_MOD = 6
EPS = 1e-6
IN_SIZES = (S5_WIDTH, SSD_WIDTH, SSD_CONV_CH, SSD_HEADS, LRU_WIDTH, LRU_WIDTH,
            RET_QK_WIDTH, RET_QK_WIDTH, RET_WIDTH, RET_WIDTH, N_BRANCH * D_MODEL)
N_IN = sum(IN_SIZES)

kernel_name = 'hybrid_gated_parallel_mixer_trunk'


def split_cols(h, sizes):
    cuts = [int(v) for v in np.cumsum(sizes)[:-1]]
    return jnp.split(h, cuts, axis=-1)


def rmsnorm(x, w):
    xf = x.astype(jnp.float32)
    y = xf * lax.rsqrt(jnp.mean(xf * xf, axis=-1, keepdims=True) + EPS)
    return (y * w.astype(jnp.float32)).astype(x.dtype)


def causal_dwconv(x, w, b):
    k_w = w.shape[0]
    y = lax.conv_general_dilated(x, w[:, None, :].astype(x.dtype), window_strides=(1,),
                                 padding=[(k_w - 1, 0)],
                                 dimension_numbers=('NWC', 'WIO', 'NWC'),
                                 feature_group_count=x.shape[-1])
    return y + b.astype(x.dtype)


def chunk_recurrence(decay, inc):
    def step(h, xs):
        d, s = xs
        return d * h + s, h
    _, h_in = lax.scan(step, jnp.zeros_like(inc[0]), (decay, inc))
    return h_in


def apply_rope(x, positions):
    half = x.shape[-1] // 2
    inv_freq = ROPE_BASE ** (-jnp.arange(half, dtype=jnp.float32) / half)
    ang = positions.astype(jnp.float32)[..., None] * inv_freq
    cos = jnp.cos(ang)[:, :, None, :]
    sin = jnp.sin(ang)[:, :, None, :]
    x1, x2 = x[..., :half], x[..., half:]
    return jnp.concatenate([x1 * cos - x2 * sin, x1 * sin + x2 * cos], axis=-1)


def s5_mixer(u, lam_re, lam_im, log_dt, b_re, b_im, c_re, c_im, d_skip, glu_w):
    bsz, seq, _ = u.shape
    f32 = jnp.float32
    uf = u.astype(f32)
    ug = uf.reshape(bsz, seq, S5_GROUPS, S5_GROUP)
    lr = jnp.minimum(lam_re.astype(f32), -1e-4)
    li = lam_im.astype(f32)
    dt = jnp.exp(log_dt.astype(f32))[:, None]
    mag = jnp.exp(lr * dt)
    ar = mag * jnp.cos(li * dt)
    ai = mag * jnp.sin(li * dt)
    den = lr * lr + li * li
    cr = ((ar - 1.0) * lr + ai * li) / den
    ci = (ai * lr - (ar - 1.0) * li) / den
    br, bi = b_re.astype(f32), b_im.astype(f32)
    bbar_re = cr[..., None] * br - ci[..., None] * bi
    bbar_im = cr[..., None] * bi + ci[..., None] * br
    bu_re = jnp.einsum('gpc,blgc->blgp', bbar_re, ug)
    bu_im = jnp.einsum('gpc,blgc->blgp', bbar_im, ug)
    a_re = jnp.broadcast_to(ar, bu_re.shape)
    a_im = jnp.broadcast_to(ai, bu_im.shape)

    def combine(e1, e2):
        a1r, a1i, b1r, b1i = e1
        a2r, a2i, b2r, b2i = e2
        return (a2r * a1r - a2i * a1i, a2r * a1i + a2i * a1r,
                a2r * b1r - a2i * b1i + b2r, a2r * b1i + a2i * b1r + b2i)

    _, _, xr, xi = lax.associative_scan(combine, (a_re, a_im, bu_re, bu_im), axis=1)
    y = (jnp.einsum('gcp,blgp->blgc', c_re.astype(f32), xr)
         - jnp.einsum('gcp,blgp->blgc', c_im.astype(f32), xi))
    y = y.reshape(bsz, seq, S5_WIDTH) + d_skip.astype(f32) * uf
    y = jax.nn.gelu(y).astype(u.dtype)
    return y * jax.nn.sigmoid(y @ glu_w)


def ssd_mixer(z, xbc, dt_raw, conv_w, conv_b, dt_bias, a_log, d_skip, norm_w):
    bsz, seq, _ = z.shape
    f32 = jnp.float32
    nc = seq // SSD_CHUNK
    rep = SSD_HEADS // SSD_GROUPS
    xbc = jax.nn.silu(causal_dwconv(xbc, conv_w, conv_b))
    xs, bm, cm = split_cols(xbc, (SSD_WIDTH, SSD_GROUPS * SSD_STATE, SSD_GROUPS * SSD_STATE))
    dt = jax.nn.softplus(dt_raw.astype(f32) + dt_bias.astype(f32))
    a = -jnp.exp(a_log.astype(f32)).reshape(SSD_GROUPS, rep)
    xh = xs.astype(f32).reshape(bsz, nc, SSD_CHUNK, SSD_GROUPS, rep, SSD_HEAD_DIM)
    bm = bm.astype(f32).reshape(bsz, nc, SSD_CHUNK, SSD_GROUPS, SSD_STATE)
    cm = cm.astype(f32).reshape(bsz, nc, SSD_CHUNK, SSD_GROUPS, SSD_STATE)
    dt = dt.reshape(bsz, nc, SSD_CHUNK, SSD_GROUPS, rep)
    xdt = xh * dt[..., None]
    a_cs = jnp.cumsum(dt * a, axis=2)
    idx = jnp.arange(SSD_CHUNK)
    causal = (idx[:, None] >= idx[None, :])[None, None, :, :, None, None]
    seg = a_cs[:, :, :, None] - a_cs[:, :, None, :]
    decay_in = jnp.exp(jnp.where(causal, seg, -jnp.inf))
    cb = jnp.einsum('bclgn,bcsgn->bclsg', cm, bm)
    y_diag = jnp.einsum('bclsg,bclsgr,bcsgrp->bclgrp', cb, decay_in, xdt)
    decay_out = jnp.exp(a_cs[:, :, -1:] - a_cs)
    chunk_states = jnp.einsum('bcsgn,bcsgr,bcsgrp->bcgrpn', bm, decay_out, xdt)
    chunk_decay = jnp.exp(a_cs[:, :, -1])
    h_in = chunk_recurrence(jnp.moveaxis(chunk_decay, 1, 0)[..., None, None],
                            jnp.moveaxis(chunk_states, 1, 0))
    h_in = jnp.moveaxis(h_in, 0, 1)
    y_off = jnp.einsum('bclgn,bcgrpn,bclgr->bclgrp', cm, h_in, jnp.exp(a_cs))
    y = (y_diag + y_off).reshape(bsz, seq, SSD_HEADS, SSD_HEAD_DIM)
    y = y + d_skip.astype(f32)[:, None] * xs.astype(f32).reshape(bsz, seq, SSD_HEADS, SSD_HEAD_DIM)
    y = y.reshape(bsz, seq, SSD_WIDTH) * jax.nn.silu(z.astype(f32))
    return rmsnorm(y, norm_w).astype(z.dtype)


def rglru_mixer(xg, xr, conv_w, conv_b, wa, ba, wx, bx, lam):
    bsz, seq, _ = xr.shape
    f32 = jnp.float32
    gate = jax.nn.gelu(xg)
    xr = causal_dwconv(xr, conv_w, conv_b)
    xb = xr.reshape(bsz, seq, LRU_BLOCKS, LRU_BLOCK)
    r = jax.nn.sigmoid(jnp.einsum('blnc,ncd->blnd', xb, wa).reshape(bsz, seq, LRU_WIDTH) + ba)
    i = jax.nn.sigmoid(jnp.einsum('blnc,ncd->blnd', xb, wx).reshape(bsz, seq, LRU_WIDTH) + bx)
    log_a = -LRU_C * r.astype(f32) * jax.nn.softplus(-lam.astype(f32))
    a = jnp.exp(log_a)
    inp = jnp.sqrt(-jnp.expm1(2.0 * log_a)) * (i * xr).astype(f32)

    def combine(e1, e2):
        a1, b1 = e1
        a2, b2 = e2
        return a1 * a2, a2 * b1 + b2

    _, h = lax.associative_scan(combine, (a, inp), axis=1)
    return h.astype(xr.dtype) * gate


def retention_mixer(q, k, v, g, positions, gn_w):
    bsz, seq, _ = q.shape
    f32 = jnp.float32
    out_dtype = g.dtype
    nc = seq // RET_CHUNK
    q = apply_rope(q.astype(f32).reshape(bsz, seq, RET_HEADS, RET_KEY_DIM), positions)
    k = apply_rope(k.astype(f32).reshape(bsz, seq, RET_HEADS, RET_KEY_DIM), positions) * RET_KEY_DIM ** -0.5
    v = v.astype(f32).reshape(bsz, seq, RET_HEADS, RET_VAL_DIM)
    log_gamma = jnp.log1p(-jnp.exp2(-5.0 - jnp.arange(RET_HEADS, dtype=f32)))
    pos = jnp.arange(RET_CHUNK, dtype=f32)
    diff = pos[:, None] - pos[None, :]
    mask = diff >= 0
    intra = jnp.where(mask, jnp.exp(jnp.where(mask, diff, 0.0)[None] * log_gamma[:, None, None]), 0.0)
    qc = q.reshape(bsz, nc, RET_CHUNK, RET_HEADS, RET_KEY_DIM)
    kc = k.reshape(bsz, nc, RET_CHUNK, RET_HEADS, RET_KEY_DIM)
    vc = v.reshape(bsz, nc, RET_CHUNK, RET_HEADS, RET_VAL_DIM)
    scores = jnp.einsum('bcihd,bcjhd->bchij', qc, kc) * intra
    inner = jnp.einsum('bchij,bcjhe->bcihe', scores, vc)
    zeta = jnp.exp((RET_CHUNK - 1.0 - pos)[None, :] * log_gamma[:, None])
    kv = jnp.einsum('bcjhd,hj,bcjhe->bchde', kc, zeta, vc)
    chunk_decay = jnp.broadcast_to(jnp.exp(RET_CHUNK * log_gamma)[None, None, :, None, None],
                                   (nc, 1, RET_HEADS, 1, 1))
    r_in = jnp.moveaxis(chunk_recurrence(chunk_decay, jnp.moveaxis(kv, 1, 0)), 0, 1)
    xi = jnp.exp((pos + 1.0)[None, :] * log_gamma[:, None])
    cross = jnp.einsum('bcihd,bchde,hi->bcihe', qc, r_in, xi)
    o = (inner + cross).reshape(bsz, seq, RET_HEADS, RET_VAL_DIM)
    mu = jnp.mean(o, axis=-1, keepdims=True)
    var = jnp.mean(jnp.square(o - mu), axis=-1, keepdims=True)
    o = ((o - mu) * lax.rsqrt(var + EPS)).reshape(bsz, seq, RET_WIDTH) * gn_w.astype(f32)
    return (jax.nn.silu(g.astype(f32)) * o).astype(out_dtype)


def conv_ffn(h, w_up, conv_w, conv_b, w_down):
    u = causal_dwconv(h @ w_up, conv_w, conv_b)
    val, gate = jnp.split(u, 2, axis=-1)
    return (jax.nn.silu(gate) * val) @ w_down


def setup_inputs(seed: int = 0) -> dict:
    key = jax.random.key(seed)
    keys = jax.random.split(key, 48)
    ks = iter([keys[j] for j in range(48)])
    f32 = jnp.float32
    L = DEPTH

    def nrm(shape, scale):
        return scale * jax.random.normal(next(ks), shape, f32)

    def unif(shape, lo, hi):
        return jax.random.uniform(next(ks), shape, f32, lo, hi)

    x = nrm((BATCH, SEQ, D_MODEL), 1.0)
    c = nrm((BATCH, D_MODEL), 1.0)
    offs = jax.random.randint(next(ks), (BATCH, 1), 0, 1024, jnp.int32)
    positions = (jnp.arange(SEQ, dtype=jnp.int32)[None, :] + offs).astype(jnp.int32)
    ada_w = nrm((L, D_MODEL, N_MOD * D_MODEL), 0.3 * D_MODEL ** -0.5)
    ada_b = nrm((L, N_MOD * D_MODEL), 0.02)
    norm1_w = 1.0 + nrm((L, D_MODEL), 0.02)
    norm2_w = 1.0 + nrm((L, D_MODEL), 0.02)
    w_in = nrm((L, D_MODEL, N_IN), D_MODEL ** -0.5)
    s5_lam_re = -0.5 + nrm((L, S5_GROUPS, S5_STATE), 0.01)
    s5_lam_im = math.pi * jnp.arange(S5_STATE, dtype=f32) + nrm((L, S5_GROUPS, S5_STATE), 0.01)
    s5_log_dt = unif((L, S5_GROUPS), math.log(1e-3), math.log(1e-1))
    s5_b_re = nrm((L, S5_GROUPS, S5_STATE, S5_GROUP), (2.0 * S5_GROUP) ** -0.5)
    s5_b_im = nrm((L, S5_GROUPS, S5_STATE, S5_GROUP), (2.0 * S5_GROUP) ** -0.5)
    s5_c_re = nrm((L, S5_GROUPS, S5_GROUP, S5_STATE), (2.0 * S5_STATE) ** -0.5)
    s5_c_im = nrm((L, S5_GROUPS, S5_GROUP, S5_STATE), (2.0 * S5_STATE) ** -0.5)
    s5_d = nrm((L, S5_WIDTH), 0.5)
    s5_glu_w = nrm((L, S5_WIDTH, S5_WIDTH), S5_WIDTH ** -0.5)
    ssd_conv_w = nrm((L, SSD_CONV, SSD_CONV_CH), SSD_CONV ** -0.5)
    ssd_conv_b = nrm((L, SSD_CONV_CH), 0.01)
    dt0 = jnp.exp(unif((L, SSD_HEADS), math.log(1e-3), math.log(1e-1)))
    ssd_dt_bias = dt0 + jnp.log(-jnp.expm1(-dt0))
    ssd_a_log = jnp.log(unif((L, SSD_HEADS), 1.0, 16.0))
    ssd_d = 1.0 + nrm((L, SSD_HEADS), 0.1)
    ssd_norm_w = 1.0 + nrm((L, SSD_WIDTH), 0.02)
    lru_conv_w = nrm((L, LRU_CONV, LRU_WIDTH), LRU_CONV ** -0.5)
    lru_conv_b = nrm((L, LRU_WIDTH), 0.01)
    lru_wa = nrm((L, LRU_BLOCKS, LRU_BLOCK, LRU_BLOCK), LRU_BLOCK ** -0.5)
    lru_ba = nrm((L, LRU_WIDTH), 0.01)
    lru_wx = nrm((L, LRU_BLOCKS, LRU_BLOCK, LRU_BLOCK), LRU_BLOCK ** -0.5)
    lru_bx = nrm((L, LRU_WIDTH), 0.01)
    a0 = unif((L, LRU_WIDTH), 0.9, 0.999) ** (1.0 / LRU_C)
    lru_lambda = jnp.log(a0) - jnp.log1p(-a0)
    ret_gn_w = 1.0 + nrm((L, RET_WIDTH), 0.02)
    w_br_a = nrm((L, S5_WIDTH, D_MODEL), S5_WIDTH ** -0.5)
    w_br_b = nrm((L, SSD_WIDTH, D_MODEL), SSD_WIDTH ** -0.5)
    w_br_c = nrm((L, LRU_WIDTH, D_MODEL), LRU_WIDTH ** -0.5)
    w_br_d = nrm((L, RET_WIDTH, D_MODEL), RET_WIDTH ** -0.5)
    w_out = nrm((L, D_MODEL, D_MODEL), D_MODEL ** -0.5)
    ffn_w_up = nrm((L, D_MODEL, 2 * D_FF), D_MODEL ** -0.5)
    ffn_conv_w = nrm((L, FFN_CONV, 2 * D_FF), FFN_CONV ** -0.5)
    ffn_conv_b = nrm((L, 2 * D_FF), 0.01)
    ffn_w_down = nrm((L, D_FF, D_MODEL), D_FF ** -0.5)
    final_norm_w = 1.0 + nrm((D_MODEL,), 0.02)
    return {'x': x, 'c': c, 'positions': positions, 'ada_w': ada_w, 'ada_b': ada_b,
            'norm1_w': norm1_w, 'norm2_w': norm2_w, 'w_in': w_in,
            's5_lam_re': s5_lam_re, 's5_lam_im': s5_lam_im, 's5_log_dt': s5_log_dt,
            's5_b_re': s5_b_re, 's5_b_im': s5_b_im, 's5_c_re': s5_c_re, 's5_c_im': s5_c_im,
            's5_d': s5_d, 's5_glu_w': s5_glu_w,
            'ssd_conv_w': ssd_conv_w, 'ssd_conv_b': ssd_conv_b, 'ssd_dt_bias': ssd_dt_bias,
            'ssd_a_log': ssd_a_log, 'ssd_d': ssd_d, 'ssd_norm_w': ssd_norm_w,
            'lru_conv_w': lru_conv_w, 'lru_conv_b': lru_conv_b, 'lru_wa': lru_wa, 'lru_ba': lru_ba,
            'lru_wx': lru_wx, 'lru_bx': lru_bx, 'lru_lambda': lru_lambda,
            'ret_gn_w': ret_gn_w,
            'w_br_a': w_br_a, 'w_br_b': w_br_b, 'w_br_c': w_br_c, 'w_br_d': w_br_d, 'w_out': w_out,
            'ffn_w_up': ffn_w_up, 'ffn_conv_w': ffn_conv_w, 'ffn_conv_b': ffn_conv_b,
            'ffn_w_down': ffn_w_down, 'final_norm_w': final_norm_w}


def reference(x, c, positions, ada_w, ada_b, norm1_w, norm2_w, w_in,
              s5_lam_re, s5_lam_im, s5_log_dt, s5_b_re, s5_b_im, s5_c_re, s5_c_im, s5_d, s5_glu_w,
              ssd_conv_w, ssd_conv_b, ssd_dt_bias, ssd_a_log, ssd_d, ssd_norm_w,
              lru_conv_w, lru_conv_b, lru_wa, lru_ba, lru_wx, lru_bx, lru_lambda,
              ret_gn_w, w_br_a, w_br_b, w_br_c, w_br_d, w_out,
              ffn_w_up, ffn_conv_w, ffn_conv_b, ffn_w_down, final_norm_w):
    bsz, seq = x.shape[0], x.shape[1]
    for l in range(DEPTH):
        mod = c @ ada_w[l] + ada_b[l]
        sh1, sc1, g1, sh2, sc2, g2 = [m[:, None, :] for m in jnp.split(mod, N_MOD, axis=-1)]
        h = rmsnorm(x, norm1_w[l]) * (1.0 + sc1) + sh1
        proj = h @ w_in[l]
        (u_s5, z_ssd, xbc_ssd, dt_ssd, xg_lru, xr_lru,
         q_ret, k_ret, v_ret, g_ret, gate_logits) = split_cols(proj, IN_SIZES)
        y_a = s5_mixer(u_s5, s5_lam_re[l], s5_lam_im[l], s5_log_dt[l], s5_b_re[l], s5_b_im[l],
                       s5_c_re[l], s5_c_im[l], s5_d[l], s5_glu_w[l])
        y_b = ssd_mixer(z_ssd, xbc_ssd, dt_ssd, ssd_conv_w[l], ssd_conv_b[l], ssd_dt_bias[l],
                        ssd_a_log[l], ssd_d[l], ssd_norm_w[l])
        y_c = rglru_mixer(xg_lru, xr_lru, lru_conv_w[l], lru_conv_b[l], lru_wa[l], lru_ba[l],
                          lru_wx[l], lru_bx[l], lru_lambda[l])
        y_d = retention_mixer(q_ret, k_ret, v_ret, g_ret, positions, ret_gn_w[l])
        gates = jax.nn.sigmoid(gate_logits.reshape(bsz, seq, N_BRANCH, D_MODEL))
        merged = (gates[:, :, 0] * (y_a @ w_br_a[l]) + gates[:, :, 1] * (y_b @ w_br_b[l])
                  + gates[:, :, 2] * (y_c @ w_br_c[l]) + gates[:, :, 3] * (y_d @ w_br_d[l]))
        x = x + g1 * (merged @ w_out[l])
        h = rmsnorm(x, norm2_w[l]) * (1.0 + sc2) + sh2
        x = x + g2 * conv_ffn(h, ffn_w_up[l], ffn_conv_w[l], ffn_conv_b[l], ffn_w_down[l])
    return rmsnorm(x, final_norm_w)
```

```python
import functools
import math

import numpy as np
import jax
import jax.numpy as jnp
from jax import lax
from jax.experimental import pallas as pl
from jax.experimental.pallas import tpu as pltpu

F32 = jnp.float32
BF16 = jnp.bfloat16

D_MODEL = 2048
S5_WIDTH = 768
S5_GROUP = 16
S5_GROUPS = S5_WIDTH // S5_GROUP
S5_STATE = 64
S5_NSTATE = S5_GROUPS * S5_STATE
S5_BLOCK_IN = 256
S5_BLOCK_ST = 1024
S5_NBLOCK = S5_WIDTH // S5_BLOCK_IN
SSD_HEADS = 16
SSD_HEAD_DIM = 64
SSD_WIDTH = SSD_HEADS * SSD_HEAD_DIM
SSD_GROUPS = 4
SSD_STATE = 128
SSD_CONV = 4
SSD_CONV_CH = SSD_WIDTH + 2 * SSD_GROUPS * SSD_STATE
LRU_WIDTH = 1024
LRU_BLOCKS = 8
LRU_BLOCK = LRU_WIDTH // LRU_BLOCKS
LRU_CONV = 4
LRU_C = 8.0
RET_HEADS = 8
RET_KEY_DIM = 64
RET_VAL_DIM = 128
RET_QK_WIDTH = RET_HEADS * RET_KEY_DIM
RET_WIDTH = RET_HEADS * RET_VAL_DIM
ROPE_BASE = 10000.0
N_BRANCH = 4
D_FF = 5632
FFN_CONV = 3
N_MOD = 6
EPS = 1e-6
IN_SIZES = (S5_WIDTH, SSD_WIDTH, SSD_CONV_CH, SSD_HEADS, LRU_WIDTH, LRU_WIDTH,
            RET_QK_WIDTH, RET_QK_WIDTH, RET_WIDTH, RET_WIDTH, N_BRANCH * D_MODEL)

LANE = 128
SUBLANE = 8
BF16_ROWS = 16
VMEM_LIMIT_BYTES = 56 * 1024 * 1024
SCAN_CHUNK = 128
HALO = 8


def _cparams(n_axes):
    return pltpu.CompilerParams(dimension_semantics=("arbitrary",) * n_axes,
                                vmem_limit_bytes=VMEM_LIMIT_BYTES)


def _tile(n, pref):
    t = min(n, pref)
    assert n % t == 0, (n, pref)
    return t


def _dot(a, b):
    return jnp.dot(a, b, preferred_element_type=F32)


def _sigmoid(x):
    return jax.nn.sigmoid(x)


def _silu(x):
    return x * jax.nn.sigmoid(x)


def _gelu_tanh(x):
    return 0.5 * x * (1.0 + jnp.tanh(math.sqrt(2.0 / math.pi) * (x + 0.044715 * (x * x * x))))


def _softplus(x):
    return jnp.maximum(x, 0.0) + jnp.log1p(jnp.exp(-jnp.abs(x)))


def _shift_rows(x, s, fill):
    row = lax.broadcasted_iota(jnp.int32, x.shape, 0)
    return jnp.where(row >= s, pltpu.roll(x, s, 0), fill)


def _scan_levels(n):
    return [1 << k for k in range(int(math.log2(n)))]


def _ada_kernel(c_ref, w_ref, b_ref, o_ref):
    o_ref[0] = _dot(c_ref[...].astype(BF16), w_ref[0].astype(BF16)) + b_ref[0]


def _ada_all(c_pad, ada_w, ada_b):
    depth, d, n = ada_w.shape
    tn = _tile(n, 1024)
    return pl.pallas_call(
        _ada_kernel,
        grid=(depth, n // tn),
        in_specs=[pl.BlockSpec((SUBLANE, d), lambda l, j: (0, 0)),
                  pl.BlockSpec((1, d, tn), lambda l, j: (l, 0, j)),
                  pl.BlockSpec((1, 1, tn), lambda l, j: (l, 0, j))],
        out_specs=pl.BlockSpec((1, SUBLANE, tn), lambda l, j: (l, 0, j)),
        out_shape=jax.ShapeDtypeStruct((depth, SUBLANE, n), F32),
        compiler_params=_cparams(2),
        name="ada_mod",
    )(c_pad, ada_w, ada_b.reshape(depth, 1, n))


def _rms(x, w):
    return x * lax.rsqrt(jnp.mean(x * x, axis=-1, keepdims=True) + EPS) * w


def _norm_mod_kernel(x_ref, w_ref, sc_ref, sh_ref, o_ref):
    o_ref[0] = (_rms(x_ref[0], w_ref[...]) * (1.0 + sc_ref[0]) + sh_ref[0]).astype(o_ref.dtype)


def _norm_kernel(x_ref, w_ref, o_ref):
    o_ref[0] = _rms(x_ref[0], w_ref[...]).astype(o_ref.dtype)


def _norm_mod(x, w, mod, sc_idx, sh_idx):
    b, l, d = x.shape
    tm = _tile(l, 1024)
    return pl.pallas_call(
        _norm_mod_kernel,
        grid=(b, l // tm),
        in_specs=[pl.BlockSpec((1, tm, d), lambda bi, i: (bi, i, 0)),
                  pl.BlockSpec((1, d), lambda bi, i: (0, 0)),
                  pl.BlockSpec((1, 1, d), lambda bi, i: (bi, 0, sc_idx)),
                  pl.BlockSpec((1, 1, d), lambda bi, i: (bi, 0, sh_idx))],
        out_specs=pl.BlockSpec((1, tm, d), lambda bi, i: (bi, i, 0)),
        out_shape=jax.ShapeDtypeStruct((b, l, d), BF16),
        compiler_params=_cparams(2),
        name="norm_mod",
    )(x, w.reshape(1, d), mod, mod)


def _final_norm(x, w):
    b, l, d = x.shape
    tm = _tile(l, 1024)
    return pl.pallas_call(
        _norm_kernel,
        grid=(b, l // tm),
        in_specs=[pl.BlockSpec((1, tm, d), lambda bi, i: (bi, i, 0)),
                  pl.BlockSpec((1, d), lambda bi, i: (0, 0))],
        out_specs=pl.BlockSpec((1, tm, d), lambda bi, i: (bi, i, 0)),
        out_shape=jax.ShapeDtypeStruct((b, l, d), F32),
        compiler_params=_cparams(2),
        name="final_norm",
    )(x, w.reshape(1, d))


def _matmul_kernel(a_ref, w_ref, o_ref):
    o_ref[0] = _dot(a_ref[0], w_ref[...]).astype(o_ref.dtype)


def _matmul(a, w, out_dtype):
    b, l, k = a.shape
    n = w.shape[1]
    tm = _tile(l, 1024)
    tn = _tile(n, 1024)
    return pl.pallas_call(
        _matmul_kernel,
        grid=(n // tn, b, l // tm),
        in_specs=[pl.BlockSpec((1, tm, k), lambda j, bi, i: (bi, i, 0)),
                  pl.BlockSpec((k, tn), lambda j, bi, i: (0, j))],
        out_specs=pl.BlockSpec((1, tm, tn), lambda j, bi, i: (bi, i, j)),
        out_shape=jax.ShapeDtypeStruct((b, l, n), out_dtype),
        compiler_params=_cparams(3),
        name="proj",
    )(a, w)


def _matmul_res_kernel(a_ref, w_ref, x_ref, g_ref, o_ref):
    o_ref[0] = x_ref[0] + g_ref[0] * _dot(a_ref[0], w_ref[...])


def _matmul_residual(a, w, x, mod, g_idx):
    b, l, k = a.shape
    n = w.shape[1]
    tm = _tile(l, 1024)
    tn = _tile(n, 1024)
    nj = n // tn
    return pl.pallas_call(
        _matmul_res_kernel,
        grid=(nj, b, l // tm),
        in_specs=[pl.BlockSpec((1, tm, k), lambda j, bi, i: (bi, i, 0)),
                  pl.BlockSpec((k, tn), lambda j, bi, i: (0, j)),
                  pl.BlockSpec((1, tm, tn), lambda j, bi, i: (bi, i, j)),
                  pl.BlockSpec((1, 1, tn), lambda j, bi, i: (bi, 0, g_idx * nj + j))],
        out_specs=pl.BlockSpec((1, tm, tn), lambda j, bi, i: (bi, i, j)),
        out_shape=jax.ShapeDtypeStruct((b, l, n), F32),
        compiler_params=_cparams(3),
        name="out_proj_residual",
    )(a, w, x, mod)


def _conv_rows(ext_ref, tail_ref, x, w_ref, b_ref, taps):
    q = x.shape[0]
    ext_ref[0:HALO, :] = tail_ref[...]
    ext_ref[HALO:HALO + q, :] = x
    tail_ref[...] = x[q - HALO:q, :]
    acc = b_ref[...] + w_ref[taps - 1:taps, :] * x
    for k in range(taps - 1):
        off = HALO - (taps - 1) + k
        acc = acc + w_ref[k:k + 1, :] * ext_ref[off:off + q, :]
    return acc


def _s5_prep_kernel(lr_ref, li_ref, ldt_ref, br_ref, bi_ref, bbr_ref, bbi_ref, apr_ref, api_ref):
    lr = jnp.minimum(lr_ref[0], -1e-4)
    li = li_ref[0]
    dt = jnp.exp(ldt_ref[0])
    mag = jnp.exp(lr * dt)
    ar = mag * jnp.cos(li * dt)
    ai = mag * jnp.sin(li * dt)
    den = lr * lr + li * li
    cr = ((ar - 1.0) * lr + ai * li) / den
    ci = (ai * lr - (ar - 1.0) * li) / den
    br = br_ref[0]
    bi = bi_ref[0]
    bbr_ref[0] = cr * br - ci * bi
    bbi_ref[0] = cr * bi + ci * br
    g = ar.shape[0]
    for k in range(apr_ref.shape[1] // g):
        apr_ref[0, k * g:(k + 1) * g] = ar
        api_ref[0, k * g:(k + 1) * g] = ai
        ar, ai = ar * ar - ai * ai, 2.0 * ar * ai


def _s5_prep(lam_re, lam_im, log_dt, b_re, b_im, n_lev):
    depth = lam_re.shape[0]
    g, p, c = S5_GROUPS, S5_STATE, S5_GROUP
    spec_gp = pl.BlockSpec((1, g, 1, p), lambda l: (l, 0, 0, 0))
    spec_b = pl.BlockSpec((1, g, c, p), lambda l: (l, 0, 0, 0))
    spec_a = pl.BlockSpec((1, n_lev * g, 1, p), lambda l: (l, 0, 0, 0))
    return pl.pallas_call(
        _s5_prep_kernel,
        grid=(depth,),
        in_specs=[spec_gp, spec_gp, pl.BlockSpec((1, g, 1, 1), lambda l: (l, 0, 0, 0)), spec_b, spec_b],
        out_specs=[spec_b, spec_b, spec_a, spec_a],
        out_shape=[jax.ShapeDtypeStruct((depth, g, c, p), F32)] * 2
        + [jax.ShapeDtypeStruct((depth, n_lev * g, 1, p), F32)] * 2,
        compiler_params=_cparams(1),
        name="s5_discretize",
    )(lam_re.reshape(depth, g, 1, p), lam_im.reshape(depth, g, 1, p), log_dt.reshape(depth, g, 1, 1),
      jnp.swapaxes(b_re, -1, -2), jnp.swapaxes(b_im, -1, -2))


def _s5_kernel(u_ref, bre_ref, bim_ref, cm_ref, apr_ref, api_ref, d_ref, glu_ref,
               o_ref, sr_ref, si_ref, y_ref):
    q = u_ref.shape[1]

    @pl.when(pl.program_id(1) == 0)
    def _():
        sr_ref[...] = jnp.zeros_like(sr_ref)
        si_ref[...] = jnp.zeros_like(si_ref)

    u = u_ref[0]
    ub = u.astype(BF16)
    levels = _scan_levels(q)
    row0 = lax.broadcasted_iota(jnp.int32, (q, LANE), 0) == 0
    for j in range(S5_NBLOCK):
        uj = ub[:, j * S5_BLOCK_IN:(j + 1) * S5_BLOCK_IN]
        bur = _dot(uj, bre_ref[j])
        bui = _dot(uj, bim_ref[j])
        xr_parts, xi_parts = [], []
        for s in range(S5_BLOCK_ST // LANE):
            c0 = j * S5_BLOCK_ST + s * LANE
            xr = bur[:, s * LANE:(s + 1) * LANE]
            xi = bui[:, s * LANE:(s + 1) * LANE]
            a_r = apr_ref[0:1, c0:c0 + LANE]
            a_i = api_ref[0:1, c0:c0 + LANE]
            pr = sr_ref[0:1, c0:c0 + LANE]
            pi = si_ref[0:1, c0:c0 + LANE]
            xr = xr + jnp.where(row0, a_r * pr - a_i * pi, 0.0)
            xi = xi + jnp.where(row0, a_r * pi + a_i * pr, 0.0)
            for k, sh in enumerate(levels):
                a_r = apr_ref[k:k + 1, c0:c0 + LANE]
                a_i = api_ref[k:k + 1, c0:c0 + LANE]
                tr = _shift_rows(xr, sh, 0.0)
                ti = _shift_rows(xi, sh, 0.0)
                xr, xi = xr + a_r * tr - a_i * ti, xi + a_r * ti + a_i * tr
            sr_ref[0:1, c0:c0 + LANE] = xr[q - 1:q, :]
            si_ref[0:1, c0:c0 + LANE] = xi[q - 1:q, :]
            xr_parts.append(xr.astype(BF16))
            xi_parts.append(xi.astype(BF16))
        xcat = jnp.concatenate(xr_parts + xi_parts, axis=-1)
        y_ref[:, j * S5_BLOCK_IN:(j + 1) * S5_BLOCK_IN] = _dot(xcat, cm_ref[j])
    y = y_ref[...] + d_ref[...] * u
    y = _gelu_tanh(y)
    o_ref[0] = (y * _sigmoid(_dot(y.astype(BF16), glu_ref[...]))).astype(o_ref.dtype)


def _s5_mixer(u_arr, u_blk, bre, bim, cmat, apr, api, d_skip, glu_w):
    b, l, _ = u_arr.shape
    q = _tile(l, SCAN_CHUNK)
    n_lev = apr.shape[0]
    full = lambda *s: pl.BlockSpec(s, lambda bi, i: (0,) * len(s))
    return pl.pallas_call(
        _s5_kernel,
        grid=(b, l // q),
        in_specs=[pl.BlockSpec((1, q, S5_WIDTH), lambda bi, i: (bi, i, u_blk)),
                  full(S5_NBLOCK, S5_BLOCK_IN, S5_BLOCK_ST),
                  full(S5_NBLOCK, S5_BLOCK_IN, S5_BLOCK_ST),
                  full(S5_NBLOCK, 2 * S5_BLOCK_ST, S5_BLOCK_IN),
                  full(n_lev, S5_NSTATE), full(n_lev, S5_NSTATE),
                  full(1, S5_WIDTH), full(S5_WIDTH, S5_WIDTH)],
        out_specs=pl.BlockSpec((1, q, S5_WIDTH), lambda bi, i: (bi, i, 0)),
        out_shape=jax.ShapeDtypeStruct((b, l, S5_WIDTH), BF16),
        scratch_shapes=[pltpu.VMEM((SUBLANE, S5_NSTATE), F32), pltpu.VMEM((SUBLANE, S5_NSTATE), F32),
                        pltpu.VMEM((q, S5_WIDTH), F32)],
        compiler_params=_cparams(2),
        name="s5_mixer",
    )(u_arr, bre, bim, cmat, apr, api, d_skip.reshape(1, S5_WIDTH), glu_w)


def _s5_block_weights(bbar_re, bbar_im, c_re, c_im):
    gpb = S5_BLOCK_IN // S5_GROUP
    eye = jnp.eye(gpb, dtype=F32)

    def b_blocks(bb):
        bb = bb.reshape(S5_NBLOCK, gpb, S5_GROUP, S5_STATE)
        out = jnp.einsum('jgcp,gh->jgchp', bb, eye)
        return out.reshape(S5_NBLOCK, S5_BLOCK_IN, S5_BLOCK_ST).astype(BF16)

    def c_blocks(cc):
        cc = cc.reshape(S5_NBLOCK, gpb, S5_GROUP, S5_STATE)
        out = jnp.einsum('jgcp,gh->jgphc', cc, eye)
        return out.reshape(S5_NBLOCK, S5_BLOCK_ST, S5_BLOCK_IN)

    cmat = jnp.concatenate([c_blocks(c_re), -c_blocks(c_im)], axis=1).astype(BF16)
    return b_blocks(bbar_re), b_blocks(bbar_im), cmat


def _expand_heads(x, e_ref):
    e = e_ref[...]
    hi = x.astype(BF16)
    r1 = x - hi.astype(F32)
    mid = r1.astype(BF16)
    lo = (r1 - mid.astype(F32)).astype(BF16)
    return _dot(hi, e) + _dot(mid, e) + _dot(lo, e)


def _ssd_kernel(xbc_ref, z_ref, dt_ref, cw_ref, cb_ref, dtb_ref, alog_ref, dsk_ref, nw_ref, e_ref,
                o_ref, ext_ref, tail_ref, h_ref):
    q = xbc_ref.shape[1]
    hp = SSD_HEAD_DIM
    gw = SSD_WIDTH // SSD_GROUPS
    hpg = SSD_HEADS // SSD_GROUPS

    @pl.when(pl.program_id(1) == 0)
    def _():
        tail_ref[...] = jnp.zeros_like(tail_ref)
        h_ref[...] = jnp.zeros_like(h_ref)

    conv = _conv_rows(ext_ref, tail_ref, xbc_ref[0], cw_ref, cb_ref, SSD_CONV)
    xbc = _silu(conv)
    xs = xbc[:, :SSD_WIDTH]
    bm = xbc[:, SSD_WIDTH:SSD_WIDTH + SSD_GROUPS * SSD_STATE]
    cm = xbc[:, SSD_WIDTH + SSD_GROUPS * SSD_STATE:]

    dt = _softplus(dt_ref[0] + dtb_ref[...])
    acs = dt * (-jnp.exp(alog_ref[...]))
    for sh in _scan_levels(q):
        acs = acs + _shift_rows(acs, sh, 0.0)
    dt_e = _expand_heads(dt, e_ref)
    acs_e = _expand_heads(acs, e_ref)
    acs_last_e = acs_e[q - 1:q, :]
    xdt = xs * dt_e
    xdtw = (xdt * jnp.exp(acs_last_e - acs_e)).astype(BF16)
    xdt_b = xdt.astype(BF16)
    acs_t = acs.T

    ti = lax.broadcasted_iota(jnp.int32, (q, q), 0)
    si = lax.broadcasted_iota(jnp.int32, (q, q), 1)
    causal = ti >= si
    lane_g = lax.broadcasted_iota(jnp.int32, (q, gw), 1)

    y_parts = []
    h_new_parts = []
    for g in range(SSD_GROUPS):
        cm_g = cm[:, g * SSD_STATE:(g + 1) * SSD_STATE].astype(BF16)
        bm_g = bm[:, g * SSD_STATE:(g + 1) * SSD_STATE]
        cb = lax.dot_general(cm_g, bm_g.astype(BF16), (((1,), (1,)), ((), ())),
                             preferred_element_type=F32)
        xg = xdt_b[:, g * gw:(g + 1) * gw]
        probs, rhs = [], []
        for r in range(hpg):
            h = g * hpg + r
            seg = acs[:, h:h + 1] - acs_t[h:h + 1, :]
            decay = jnp.exp(jnp.where(causal, seg, -jnp.inf))
            probs.append((cb * decay).astype(BF16))
            rhs.append(jnp.where((lane_g >= r * hp) & (lane_g < (r + 1) * hp), xg, 0))
        y_diag = _dot(jnp.concatenate(probs, axis=1), jnp.concatenate(rhs, axis=0))
        h_in = h_ref[:, g * gw:(g + 1) * gw]
        y_off = _dot(cm_g, h_in.astype(BF16))
        y_parts.append(y_diag + y_off * jnp.exp(acs_e[:, g * gw:(g + 1) * gw]))
        states = _dot(bm_g.T.astype(BF16), xdtw[:, g * gw:(g + 1) * gw])
        h_new_parts.append(h_in * jnp.exp(acs_last_e[:, g * gw:(g + 1) * gw]) + states)
    for g in range(SSD_GROUPS):
        h_ref[:, g * gw:(g + 1) * gw] = h_new_parts[g]

    y = jnp.concatenate(y_parts, axis=1) + dsk_ref[...] * xs
    y = y * _silu(z_ref[0])
    o_ref[0] = _rms(y, nw_ref[...]).astype(o_ref.dtype)


def _ssd_mixer(xz_arr, dt_arr, dt_blk, conv_w, conv_b, dt_bias, a_log, d_skip, norm_w):
    b, l, _ = xz_arr.shape
    q = _tile(l, SCAN_CHUNK)
    full = lambda *s: pl.BlockSpec(s, lambda bi, i: (0,) * len(s))
    pad = LANE - SSD_HEADS
    expand = np.zeros((LANE, SSD_WIDTH), np.float32)
    for h in range(SSD_HEADS):
        expand[h, h * SSD_HEAD_DIM:(h + 1) * SSD_HEAD_DIM] = 1.0
    return pl.pallas_call(
        _ssd_kernel,
        grid=(b, l // q),
        in_specs=[pl.BlockSpec((1, q, SSD_CONV_CH), lambda bi, i: (bi, i, 0)),
                  pl.BlockSpec((1, q, SSD_WIDTH), lambda bi, i: (bi, i, SSD_CONV_CH // SSD_WIDTH)),
                  pl.BlockSpec((1, q, LANE), lambda bi, i: (bi, i, dt_blk)),
                  full(SSD_CONV, SSD_CONV_CH), full(1, SSD_CONV_CH), full(1, LANE), full(1, LANE),
                  full(1, SSD_WIDTH), full(1, SSD_WIDTH), full(LANE, SSD_WIDTH)],
        out_specs=pl.BlockSpec((1, q, SSD_WIDTH), lambda bi, i: (bi, i, 0)),
        out_shape=jax.ShapeDtypeStruct((b, l, SSD_WIDTH), BF16),
        scratch_shapes=[pltpu.VMEM((HALO + q, SSD_CONV_CH), F32), pltpu.VMEM((HALO, SSD_CONV_CH), F32),
                        pltpu.VMEM((SSD_STATE, SSD_WIDTH), F32)],
        compiler_params=_cparams(2),
        name="ssd_mixer",
    )(xz_arr, xz_arr, dt_arr, conv_w, conv_b.reshape(1, -1),
      jnp.pad(dt_bias, (0, pad)).reshape(1, LANE), jnp.pad(a_log, (0, pad)).reshape(1, LANE),
      jnp.repeat(d_skip, SSD_HEAD_DIM).reshape(1, SSD_WIDTH), norm_w.reshape(1, SSD_WIDTH),
      jnp.asarray(expand, BF16))


def _lru_kernel(xg_ref, xr_ref, cw_ref, cb_ref, wa_ref, wx_ref, ba_ref, bx_ref, lam_ref,
                o_ref, ext_ref, tail_ref, h_ref):
    q = xr_ref.shape[1]

    @pl.when(pl.program_id(1) == 0)
    def _():
        tail_ref[...] = jnp.zeros_like(tail_ref)
        h_ref[...] = jnp.zeros_like(h_ref)

    xc = _conv_rows(ext_ref, tail_ref, xr_ref[0], cw_ref, cb_ref, LRU_CONV)
    xcb = xc.astype(BF16)
    sp = _softplus(-lam_ref[...])
    row0 = lax.broadcasted_iota(jnp.int32, (q, LRU_BLOCK), 0) == 0
    levels = _scan_levels(q)
    for n in range(LRU_BLOCKS):
        sl = slice(n * LRU_BLOCK, (n + 1) * LRU_BLOCK)
        xn = xc[:, sl]
        r = _sigmoid(_dot(xcb[:, sl], wa_ref[n]) + ba_ref[:, sl])
        ig = _sigmoid(_dot(xcb[:, sl], wx_ref[n]) + bx_ref[:, sl])
        log_a = -LRU_C * r * sp[:, sl]
        a = jnp.exp(log_a)
        bb = jnp.sqrt(-jnp.tanh(log_a) * (a * a + 1.0)) * (ig * xn)
        bb = bb + jnp.where(row0, a * h_ref[0:1, sl], 0.0)
        for sh in levels:
            bb = bb + a * _shift_rows(bb, sh, 0.0)
            a = a * _shift_rows(a, sh, 1.0)
        h_ref[0:1, sl] = bb[q - 1:q, :]
        o_ref[0, :, sl] = (bb * _gelu_tanh(xg_ref[0, :, sl])).astype(o_ref.dtype)


def _lru_mixer(g_arr, conv_w, conv_b, wa, ba, wx, bx, lam):
    b, l, _ = g_arr.shape
    q = _tile(l, SCAN_CHUNK)
    w = LRU_WIDTH
    full = lambda *s: pl.BlockSpec(s, lambda bi, i: (0,) * len(s))
    return pl.pallas_call(
        _lru_kernel,
        grid=(b, l // q),
        in_specs=[pl.BlockSpec((1, q, w), lambda bi, i: (bi, i, 0)),
                  pl.BlockSpec((1, q, w), lambda bi, i: (bi, i, 1)),
                  full(LRU_CONV, w), full(1, w),
                  full(LRU_BLOCKS, LRU_BLOCK, LRU_BLOCK), full(LRU_BLOCKS, LRU_BLOCK, LRU_BLOCK),
                  full(1, w), full(1, w), full(1, w)],
        out_specs=pl.BlockSpec((1, q, w), lambda bi, i: (bi, i, 0)),
        out_shape=jax.ShapeDtypeStruct((b, l, w), BF16),
        scratch_shapes=[pltpu.VMEM((HALO + q, w), F32), pltpu.VMEM((HALO, w), F32),
                        pltpu.VMEM((SUBLANE, w), F32)],
        compiler_params=_cparams(2),
        name="rglru_mixer",
    )(g_arr, g_arr, conv_w, conv_b.reshape(1, w), wa.astype(BF16), wx.astype(BF16),
      ba.reshape(1, w), bx.reshape(1, w), lam.reshape(1, w))


def _rope_table_kernel(pos_ref, inv_ref, cos_ref, sin_ref):
    ang = pos_ref[0] * inv_ref[...]
    lane = lax.broadcasted_iota(jnp.int32, ang.shape, 1)
    first = (lane % RET_KEY_DIM) < RET_KEY_DIM // 2
    cos_ref[0] = jnp.cos(ang)
    sn = jnp.sin(ang)
    sin_ref[0] = jnp.where(first, -sn, sn)


def _rope_tables(positions):
    b, l = positions.shape
    q = _tile(l, 512)
    half = RET_KEY_DIM // 2
    inv = ROPE_BASE ** (-(np.arange(LANE) % half).astype(np.float64) / half)
    spec = pl.BlockSpec((1, q, LANE), lambda bi, i: (bi, i, 0))
    return pl.pallas_call(
        _rope_table_kernel,
        grid=(b, l // q),
        in_specs=[pl.BlockSpec((1, q, 1), lambda bi, i: (bi, i, 0)),
                  pl.BlockSpec((1, LANE), lambda bi, i: (0, 0))],
        out_specs=[spec, spec],
        out_shape=[jax.ShapeDtypeStruct((b, l, LANE), F32)] * 2,
        compiler_params=_cparams(2),
        name="rope_tables",
    )(positions.astype(F32).reshape(b, l, 1), jnp.asarray(inv.reshape(1, LANE), F32))


def _ret_log_gamma(h):
    return float(np.log1p(-np.exp2(np.float32(-5.0 - h), dtype=np.float32), dtype=np.float32))


def _ret_kernel(q_ref, k_ref, v_ref, g_ref, cos_ref, sin_ref, gnw_ref, o_ref, r_ref):
    q = q_ref.shape[1]
    kd, vd = RET_KEY_DIM, RET_VAL_DIM

    @pl.when(pl.program_id(1) == 0)
    def _():
        r_ref[...] = jnp.zeros_like(r_ref)

    cosf = cos_ref[0]
    sinf = sin_ref[0]
    lane = lax.broadcasted_iota(jnp.int32, (q, LANE), 1)
    first = (lane % kd) < kd // 2
    tpos = lax.broadcasted_iota(jnp.int32, (q, LANE), 0).astype(F32)
    ti = lax.broadcasted_iota(jnp.int32, (q, q), 0)
    si = lax.broadcasted_iota(jnp.int32, (q, q), 1)
    diff = jnp.where(ti >= si, ti - si, 0).astype(F32)

    def rope(x):
        rot = jnp.where(first, pltpu.roll(x, LANE - kd // 2, 1), pltpu.roll(x, kd // 2, 1))
        return x * cosf + rot * sinf

    for s in range(RET_QK_WIDTH // LANE):
        sl = slice(s * LANE, (s + 1) * LANE)
        heads = (2 * s, 2 * s + 1)
        lg = jnp.where(lane < kd, _ret_log_gamma(heads[0]), _ret_log_gamma(heads[1]))
        qr = rope(q_ref[0, :, sl])
        kr = rope(k_ref[0, :, sl]) * (kd ** -0.5)
        qx = qr * jnp.exp((tpos + 1.0) * lg)
        kz_t = (kr * jnp.exp((q - 1.0 - tpos) * lg)).T.astype(BF16)
        kr_t = kr.T.astype(BF16)
        r_in = r_ref[sl, :].astype(BF16)
        for hh, h in enumerate(heads):
            lg_h = _ret_log_gamma(h)
            hmask = (lane >= hh * kd) & (lane < (hh + 1) * kd)
            vh = v_ref[0, :, h * vd:(h + 1) * vd].astype(BF16)
            scores = _dot(jnp.where(hmask, qr, 0.0).astype(BF16), kr_t)
            intra = jnp.where(ti >= si, jnp.exp(diff * lg_h), 0.0)
            inner = _dot((scores * intra).astype(BF16), vh)
            cross = _dot(jnp.where(hmask, qx, 0.0).astype(BF16), r_in)
            o = inner + cross
            mu = jnp.mean(o, axis=-1, keepdims=True)
            var = jnp.mean(jnp.square(o - mu), axis=-1, keepdims=True)
            o = (o - mu) * lax.rsqrt(var + EPS) * gnw_ref[:, h * vd:(h + 1) * vd]
            o_ref[0, :, h * vd:(h + 1) * vd] = (_silu(g_ref[0, :, h * vd:(h + 1) * vd]) * o).astype(o_ref.dtype)
            rows = slice(h * kd, (h + 1) * kd)
            kv = _dot(kz_t[hh * kd:(hh + 1) * kd, :], vh)
            r_ref[rows, :] = math.exp(q * lg_h) * r_ref[rows, :] + kv


def _ret_mixer(r_arr, cos_t, sin_t, gn_w):
    b, l, _ = r_arr.shape
    q = _tile(l, SCAN_CHUNK)
    qk_blk0 = 2 * RET_WIDTH // RET_QK_WIDTH
    tab = pl.BlockSpec((1, q, LANE), lambda bi, i: (bi, i, 0))
    return pl.pallas_call(
        _ret_kernel,
        grid=(b, l // q),
        in_specs=[pl.BlockSpec((1, q, RET_QK_WIDTH), lambda bi, i: (bi, i, qk_blk0)),
                  pl.BlockSpec((1, q, RET_QK_WIDTH), lambda bi, i: (bi, i, qk_blk0 + 1)),
                  pl.BlockSpec((1, q, RET_WIDTH), lambda bi, i: (bi, i, 0)),
                  pl.BlockSpec((1, q, RET_WIDTH), lambda bi, i: (bi, i, 1)),
                  tab, tab, pl.BlockSpec((1, RET_WIDTH), lambda bi, i: (0, 0))],
        out_specs=pl.BlockSpec((1, q, RET_WIDTH), lambda bi, i: (bi, i, 0)),
        out_shape=jax.ShapeDtypeStruct((b, l, RET_WIDTH), BF16),
        scratch_shapes=[pltpu.VMEM((RET_QK_WIDTH, RET_VAL_DIM), F32)],
        compiler_params=_cparams(2),
        name="retention_mixer",
    )(r_arr, r_arr, r_arr, r_arr, cos_t, sin_t, gn_w.reshape(1, RET_WIDTH))


def _merge_kernel(ya_ref, yb_ref, yc_ref, yd_ref, ga_ref, gb_ref, gc_ref, gd_ref,
                  wa_ref, wb_ref, wc_ref, wd_ref, o_ref):
    acc = _sigmoid(ga_ref[0].astype(F32)) * _dot(ya_ref[0], wa_ref[...])
    acc = acc + _sigmoid(gb_ref[0].astype(F32)) * _dot(yb_ref[0], wb_ref[...])
    acc = acc + _sigmoid(gc_ref[0].astype(F32)) * _dot(yc_ref[0], wc_ref[...])
    acc = acc + _sigmoid(gd_ref[0].astype(F32)) * _dot(yd_ref[0], wd_ref[...])
    o_ref[0] = acc.astype(o_ref.dtype)


def _merge(ys, gates, ws):
    b, l, _ = gates.shape
    d = D_MODEL
    tm = _tile(l, 512)
    tn = _tile(d, 1024)
    nj = d // tn
    y_specs = [pl.BlockSpec((1, tm, y.shape[-1]), lambda j, bi, i: (bi, i, 0)) for y in ys]
    g_specs = [pl.BlockSpec((1, tm, tn), functools.partial(lambda j, bi, i, k: (bi, i, k * nj + j), k=k))
               for k in range(N_BRANCH)]
    w_specs = [pl.BlockSpec((w.shape[0], tn), lambda j, bi, i: (0, j)) for w in ws]
    return pl.pallas_call(
        _merge_kernel,
        grid=(nj, b, l // tm),
        in_specs=y_specs + g_specs + w_specs,
        out_specs=pl.BlockSpec((1, tm, tn), lambda j, bi, i: (bi, i, j)),
        out_shape=jax.ShapeDtypeStruct((b, l, d), BF16),
        compiler_params=_cparams(3),
        name="gated_merge",
    )(*ys, gates, gates, gates, gates, *ws)


def _ffn_kernel(x_ref, xh_ref, nw_ref, sc_ref, sh_ref, g_ref, wv_ref, wg_ref, cwv_ref, cwg_ref,
                cbv_ref, cbg_ref, wd_ref, o_ref, h_ref, uv_ref, ug_ref, acc_ref):
    tm = x_ref.shape[1]
    hr = BF16_ROWS
    i = pl.program_id(1)
    j = pl.program_id(2)

    @pl.when(j == 0)
    def _():
        scale = 1.0 + sc_ref[0]
        shift = sh_ref[0]
        h_ref[hr:hr + tm, :] = (_rms(x_ref[0], nw_ref[...]) * scale + shift).astype(BF16)
        halo = _rms(xh_ref[0], nw_ref[...]) * scale + shift
        h_ref[0:hr, :] = jnp.where(i == 0, 0.0, halo).astype(BF16)

    hb = h_ref[...]
    uv_ref[...] = _dot(hb, wv_ref[...])
    ug_ref[...] = _dot(hb, wg_ref[...])

    def conv(u_ref, cw_ref, cb_ref):
        acc = cb_ref[...]
        for k in range(FFN_CONV):
            off = hr - (FFN_CONV - 1) + k
            acc = acc + cw_ref[k:k + 1, :] * u_ref[off:off + tm, :]
        return acc

    act = (_silu(conv(ug_ref, cwg_ref, cbg_ref)) * conv(uv_ref, cwv_ref, cbv_ref)).astype(BF16)
    contrib = _dot(act, wd_ref[...])

    @pl.when(j == 0)
    def _():
        acc_ref[...] = contrib

    @pl.when(j > 0)
    def _():
        acc_ref[...] += contrib

    @pl.when(j == pl.num_programs(2) - 1)
    def _():
        o_ref[0] = x_ref[0] + g_ref[0] * acc_ref[...]


def _ffn(x, norm_w, mod, w_up, conv_w, conv_b, w_down):
    b, l, d = x.shape
    tm = _tile(l, 512)
    tf = 512
    nf = D_FF // tf
    hr = BF16_ROWS
    rpb = tm // hr
    full = lambda *s: pl.BlockSpec(s, lambda bi, i, j: (0,) * len(s))
    mod_spec = lambda idx: pl.BlockSpec((1, 1, d), lambda bi, i, j: (bi, 0, idx))
    cb = conv_b.reshape(1, 2 * D_FF)
    return pl.pallas_call(
        _ffn_kernel,
        grid=(b, l // tm, nf),
        in_specs=[pl.BlockSpec((1, tm, d), lambda bi, i, j: (bi, i, 0)),
                  pl.BlockSpec((1, hr, d), lambda bi, i, j: (bi, jnp.maximum(i * rpb - 1, 0), 0)),
                  full(1, d), mod_spec(4), mod_spec(3), mod_spec(5),
                  pl.BlockSpec((d, tf), lambda bi, i, j: (0, j)),
                  pl.BlockSpec((d, tf), lambda bi, i, j: (0, nf + j)),
                  pl.BlockSpec((FFN_CONV, tf), lambda bi, i, j: (0, j)),
                  pl.BlockSpec((FFN_CONV, tf), lambda bi, i, j: (0, nf + j)),
                  pl.BlockSpec((1, tf), lambda bi, i, j: (0, j)),
                  pl.BlockSpec((1, tf), lambda bi, i, j: (0, nf + j)),
                  pl.BlockSpec((tf, d), lambda bi, i, j: (j, 0))],
        out_specs=pl.BlockSpec((1, tm, d), lambda bi, i, j: (bi, i, 0)),
        out_shape=jax.ShapeDtypeStruct((b, l, d), F32),
        scratch_shapes=[pltpu.VMEM((hr + tm, d), BF16), pltpu.VMEM((hr + tm, tf), F32),
                        pltpu.VMEM((hr + tm, tf), F32), pltpu.VMEM((tm, d), F32)],
        compiler_params=_cparams(3),
        name="conv_ffn",
    )(x, x, norm_w.reshape(1, d), mod, mod, mod, w_up, w_up, conv_w, conv_w, cb, cb, w_down)


def _split_w_in(w):
    cuts = np.cumsum((0,) + IN_SIZES)
    (u_s5, z, xbc, dt, xg, xr, q, k, v, g, gates) = [w[:, cuts[i]:cuts[i + 1]] for i in range(len(IN_SIZES))]
    pad = jnp.zeros((w.shape[0], 1024 - S5_WIDTH - SSD_HEADS), w.dtype)
    groups = (jnp.concatenate([u_s5, dt, pad], axis=1),
              jnp.concatenate([xbc, z], axis=1),
              jnp.concatenate([xg, xr], axis=1),
              jnp.concatenate([v, g, q, k], axis=1),
              gates)
    return [gw.astype(BF16) for gw in groups]


def kernel(x, c, positions, ada_w, ada_b, norm1_w, norm2_w, w_in, s5_lam_re, s5_lam_im, s5_log_dt, s5_b_re, s5_b_im, s5_c_re, s5_c_im, s5_d, s5_glu_w, ssd_conv_w, ssd_conv_b, ssd_dt_bias, ssd_a_log, ssd_d, ssd_norm_w, lru_conv_w, lru_conv_b, lru_wa, lru_ba, lru_wx, lru_bx, lru_lambda, ret_gn_w, w_br_a, w_br_b, w_br_c, w_br_d, w_out, ffn_w_up, ffn_conv_w, ffn_conv_b, ffn_w_down, final_norm_w):
    bsz, seq, d = x.shape
    depth = ada_w.shape[0]
    assert bsz <= SUBLANE and d == D_MODEL

    c_pad = jnp.pad(c, ((0, SUBLANE - bsz), (0, 0)))
    mod_all = _ada_all(c_pad, ada_w, ada_b)
    cos_t, sin_t = _rope_tables(positions)
    n_lev = int(math.log2(_tile(seq, SCAN_CHUNK)))
    bbar_re, bbar_im, apow_re, apow_im = _s5_prep(s5_lam_re, s5_lam_im, s5_log_dt, s5_b_re, s5_b_im, n_lev)

    for l in range(depth):
        mod = mod_all[l, :bsz].reshape(bsz, 1, N_MOD * d)
        w_s5, w_ssd, w_lru, w_ret, w_gate = _split_w_in(w_in[l])

        h = _norm_mod(x, norm1_w[l], mod, 1, 0)
        p_s5 = _matmul(h, w_s5, F32)
        p_ssd = _matmul(h, w_ssd, F32)
        p_lru = _matmul(h, w_lru, F32)
        p_ret = _matmul(h, w_ret, F32)
        gates = _matmul(h, w_gate, BF16)

        bre, bim, cmat = _s5_block_weights(bbar_re[l], bbar_im[l], s5_c_re[l], s5_c_im[l])
        y_a = _s5_mixer(p_s5, 0, bre, bim, cmat,
                        apow_re[l].reshape(n_lev, S5_NSTATE), apow_im[l].reshape(n_lev, S5_NSTATE),
                        s5_d[l], s5_glu_w[l].astype(BF16))
        y_b = _ssd_mixer(p_ssd, p_s5, S5_WIDTH // LANE, ssd_conv_w[l], ssd_conv_b[l], ssd_dt_bias[l],
                         ssd_a_log[l], ssd_d[l], ssd_norm_w[l])
        y_c = _lru_mixer(p_lru, lru_conv_w[l], lru_conv_b[l], lru_wa[l], lru_ba[l], lru_wx[l], lru_bx[l],
                         lru_lambda[l])
        y_d = _ret_mixer(p_ret, cos_t, sin_t, ret_gn_w[l])

        merged = _merge((y_a, y_b, y_c, y_d), gates,
                        [w.astype(BF16) for w in (w_br_a[l], w_br_b[l], w_br_c[l], w_br_d[l])])
        x = _matmul_residual(merged, w_out[l].astype(BF16), x, mod, 2)
        x = _ffn(x, norm2_w[l], mod, ffn_w_up[l].astype(BF16), ffn_conv_w[l], ffn_conv_b[l],
                 ffn_w_down[l].astype(BF16))
    return _final_norm(x, final_norm_w)
```
